```python
import math
import jax
import jax.numpy as jnp
from jax import lax
import numpy as np

D_MODEL = 1024
BATCH = 8
SEQ = 2048
DEPTH = 4
DEC_BATCH = 32
DEC_SEQ = 1
PAST_LEN = 16384
PAGE_SIZE = 128

F32 = jnp.float32
N_META = 16
N_MIXERS = 3
N_LRU = (DEPTH + 2) // 3
N_DN = (DEPTH + 1) // 3
N_MLA = DEPTH // 3
RMS_EPS = 1e-6
CONV_W = 4
LRU_WIDTH = D_MODEL
LRU_BLOCKS = 4
LRU_BW = LRU_WIDTH // LRU_BLOCKS
LRU_C = 8.0
DN_HEADS = 8
DN_DK = 128
DN_DV = 128
DN_QKV = DN_HEADS * (2 * DN_DK + DN_DV)
DN_CHUNK = 64
MLA_HEADS = 8
MLA_Q_RANK = 512
MLA_KV_RANK = 256
MLA_NOPE = 128
MLA_ROPE = 64
MLA_V = 128
MLA_SCALE = (MLA_NOPE + MLA_ROPE) ** -0.5
ROPE_THETA = 10000.0
Q_BLOCK = 128
D_FF = ((8 * D_MODEL // 3 + 255) // 256) * 256

kernel_name = 'hybrid_rglru_deltanet_mla_decoder_step'


def rmsnorm(x, g):
    xf = x.astype(F32)
    y = xf * lax.rsqrt(jnp.mean(xf * xf, axis=-1, keepdims=True) + RMS_EPS)
    return (y * g.astype(F32)).astype(x.dtype)


def l2norm(x):
    return x * lax.rsqrt(jnp.sum(x * x, axis=-1, keepdims=True) + 1e-6)


def causal_dwconv(x, buf, w):
    L = x.shape[1]
    xx = jnp.concatenate([buf.astype(x.dtype), x], axis=1)
    y = xx[:, 0:L] * w[0]
    for j in range(1, CONV_W):
        y = y + xx[:, j:j + L] * w[j]
    return y, xx[:, L:]


def rope(x, pos):
    half = MLA_ROPE // 2
    freqs = ROPE_THETA ** (-jnp.arange(half, dtype=F32) / half)
    ang = pos.astype(F32)[:, None] * freqs
    ang = ang.reshape((1, pos.shape[0]) + (1,) * (x.ndim - 3) + (half,))
    c, s = jnp.cos(ang), jnp.sin(ang)
    xf = x.astype(F32)
    x1, x2 = xf[..., :half], xf[..., half:]
    return jnp.concatenate([x1 * c - x2 * s, x1 * s + x2 * c], axis=-1).astype(x.dtype)


def lru_mixer(h, h0, buf, w_in, conv_w, conv_b, w_a, b_a, w_i, b_i, lam, w_out):
    B, L, _ = h.shape
    gx = h @ w_in
    gate = jax.nn.gelu(gx[..., :LRU_WIDTH])
    xc, new_buf = causal_dwconv(gx[..., LRU_WIDTH:], buf, conv_w)
    xc = xc + conv_b
    xb = xc.reshape(B, L, LRU_BLOCKS, LRU_BW)
    r = jax.nn.sigmoid(jnp.einsum('blnc,ncd->blnd', xb, w_a).reshape(B, L, LRU_WIDTH) + b_a)
    i = jax.nn.sigmoid(jnp.einsum('blnc,ncd->blnd', xb, w_i).reshape(B, L, LRU_WIDTH) + b_i)
    log_a = (-LRU_C * jax.nn.softplus(-lam.astype(F32))) * r.astype(F32)
    a = jnp.exp(log_a)
    u = jnp.sqrt(-jnp.expm1(2.0 * log_a)) * (i * xc).astype(F32)
    u = u.at[:, 0].add(a[:, 0] * h0.astype(F32))

    def comb(left, right):
        return (left[0] * right[0], right[0] * left[1] + right[1])

    _, hs = lax.associative_scan(comb, (a, u), axis=1)
    y = (hs.astype(h.dtype) * gate) @ w_out
    return y, hs[:, -1].astype(h.dtype), new_buf


def gated_delta_rule(q, k, v, g, beta, S0, chunk):
    B, L, H, DK = q.shape
    DV = v.shape[-1]
    N, C = L // chunk, chunk

    def blk(t):
        t = t.reshape((B, N, C, H) + t.shape[3:])
        return jnp.moveaxis(t, (1, 3), (0, 2))

    qc = blk(q) * (DK ** -0.5)
    kc, vc, gc, bc = blk(k), blk(v), blk(g), blk(beta)
    G = jnp.cumsum(gc, axis=-1)
    tril = jnp.tril(jnp.ones((C, C), bool))
    strict = jnp.tril(jnp.ones((C, C), bool), -1)
    decay = jnp.exp(jnp.where(tril, G[..., :, None] - G[..., None, :], -jnp.inf))
    kk = jnp.einsum('nbhcd,nbhsd->nbhcs', kc, kc)
    A = jnp.where(strict, bc[..., :, None] * kk * decay, 0.0) + jnp.eye(C, dtype=F32)
    rhs = jnp.concatenate([vc * bc[..., None], kc * (bc * jnp.exp(G))[..., None]], axis=-1)
    sol = lax.linalg.triangular_solve(A, rhs, left_side=True, lower=True, unit_diagonal=True)
    u, w = sol[..., :DV], sol[..., DV:]
    attn = jnp.einsum('nbhcd,nbhsd->nbhcs', qc, kc) * decay
    q_dec = qc * jnp.exp(G)[..., None]
    k_dec = kc * jnp.exp(G[..., -1:] - G)[..., None]
    g_last = G[..., -1]

    def step(S, xs):
        u_n, w_n, qd, kd, at, gl = xs
        delta = u_n - jnp.einsum('bhcd,bhdv->bhcv', w_n, S)
        o = jnp.einsum('bhcd,bhdv->bhcv', qd, S) + jnp.einsum('bhcs,bhsv->bhcv', at, delta)
        S = S * jnp.exp(gl)[..., None, None] + jnp.einsum('bhcd,bhcv->bhdv', kd, delta)
        return S, o

    S, o = lax.scan(step, S0, (u, w, q_dec, k_dec, attn, g_last))
    o = jnp.moveaxis(o, (0, 2), (1, 3)).reshape(B, L, H, DV)
    return S, o


def deltanet_mixer(h, S0, buf, lead, w_in, conv_w, a_log, dt_bias, norm_g, w_out):
    B, L, _ = h.shape
    HV = DN_HEADS * DN_DV
    proj = h @ w_in
    qkv, new_buf = causal_dwconv(proj[..., :DN_QKV], buf, conv_w)
    qkv = jax.nn.silu(qkv).astype(F32)
    z = proj[..., DN_QKV:DN_QKV + HV].reshape(B, L, DN_HEADS, DN_DV).astype(F32)
    b = proj[..., DN_QKV + HV:DN_QKV + HV + DN_HEADS].astype(F32)
    a = proj[..., DN_QKV + HV + DN_HEADS:].astype(F32)
    q = l2norm(qkv[..., :DN_HEADS * DN_DK].reshape(B, L, DN_HEADS, DN_DK))
    k = l2norm(qkv[..., DN_HEADS * DN_DK:2 * DN_HEADS * DN_DK].reshape(B, L, DN_HEADS, DN_DK))
    v = qkv[..., 2 * DN_HEADS * DN_DK:].reshape(B, L, DN_HEADS, DN_DV)
    beta = jax.nn.sigmoid(b)
    g = -jnp.exp(a_log.astype(F32)) * jax.nn.softplus(a + dt_bias.astype(F32))
    S0 = S0.astype(F32)
    if lead > 0:
        S1, o1 = gated_delta_rule(q[:, :lead], k[:, :lead], v[:, :lead], g[:, :lead], beta[:, :lead], S0, lead)
        S, o2 = gated_delta_rule(q[:, lead:], k[:, lead:], v[:, lead:], g[:, lead:], beta[:, lead:], S1, DN_CHUNK)
        o = jnp.concatenate([o1, o2], axis=1)
    else:
        S, o = gated_delta_rule(q, k, v, g, beta, S0, L)
    o = rmsnorm(o, norm_g) * jax.nn.silu(z)
    y = o.reshape(B, L, HV).astype(h.dtype) @ w_out
    return y, S.astype(h.dtype), new_buf


def mla_attend_prompt(q_lat, q_pe, ckv, kpe):
    B, L, H, _ = q_lat.shape
    nblk = -(-L // Q_BLOCK)
    Lp = nblk * Q_BLOCK

    def blocks(t):
        t = jnp.pad(t, ((0, 0), (0, Lp - L), (0, 0), (0, 0)))
        return jnp.swapaxes(t.reshape((B, nblk, Q_BLOCK) + t.shape[2:]), 0, 1)

    kpos = jnp.arange(L)

    def one_block(args):
        ql, qp, start = args
        s = (jnp.einsum('bqhc,bkc->bhqk', ql, ckv) + jnp.einsum('bqhr,bkr->bhqk', qp, kpe)).astype(F32) * MLA_SCALE
        qpos = start + jnp.arange(Q_BLOCK)
        s = jnp.where(kpos[None, :] <= qpos[:, None], s, -jnp.inf)
        p = jax.nn.softmax(s, axis=-1).astype(ckv.dtype)
        return jnp.einsum('bhqk,bkc->bqhc', p, ckv)

    out = lax.map(one_block, (blocks(q_lat), blocks(q_pe), jnp.arange(nblk) * Q_BLOCK))
    return jnp.swapaxes(out, 0, 1).reshape(B, Lp, H, MLA_KV_RANK)[:, :L]


def mla_attend_sample(q_lat, q_pe, ckv, kpe, pool_ckv, pool_kpe, page_table):
    DB, S = q_lat.shape[:2]
    ckv_past = pool_ckv[page_table].reshape(DB, -1, MLA_KV_RANK)
    kpe_past = pool_kpe[page_table].reshape(DB, -1, MLA_ROPE)
    P = ckv_past.shape[1]
    s_past = (jnp.einsum('bqhc,bkc->bhqk', q_lat, ckv_past) + jnp.einsum('bqhr,bkr->bhqk', q_pe, kpe_past)).astype(F32) * MLA_SCALE
    s_new = (jnp.einsum('bqhc,bkc->bhqk', q_lat, ckv) + jnp.einsum('bqhr,bkr->bhqk', q_pe, kpe)).astype(F32) * MLA_SCALE
    s_new = jnp.where(jnp.tril(jnp.ones((S, S), bool)), s_new, -jnp.inf)
    p = jax.nn.softmax(jnp.concatenate([s_past, s_new], axis=-1), axis=-1).astype(ckv.dtype)
    return jnp.einsum('bhqk,bkc->bqhc', p[..., :P], ckv_past) + jnp.einsum('bhqk,bkc->bqhc', p[..., P:], ckv)


def mla_mixer(h, pos0, pool_ckv, pool_kpe, page_table, w_dq, q_norm, w_uq, w_dkv, kv_norm, w_uk, w_uv, w_o):
    B, L, _ = h.shape
    pos = pos0 + jnp.arange(L, dtype=jnp.int32)
    cq = rmsnorm(h @ w_dq, q_norm)
    q = (cq @ w_uq).reshape(B, L, MLA_HEADS, MLA_NOPE + MLA_ROPE)
    q_nope, q_pe = q[..., :MLA_NOPE], rope(q[..., MLA_NOPE:], pos)
    kv = h @ w_dkv
    ckv = rmsnorm(kv[..., :MLA_KV_RANK], kv_norm)
    kpe = rope(kv[..., MLA_KV_RANK:], pos)
    q_lat = jnp.einsum('blhn,chn->blhc', q_nope, w_uk)
    if pool_ckv is None:
        out_lat = mla_attend_prompt(q_lat, q_pe, ckv, kpe)
    else:
        out_lat = mla_attend_sample(q_lat, q_pe, ckv, kpe, pool_ckv, pool_kpe, page_table)
    o = jnp.einsum('blhc,chv->blhv', out_lat, w_uv).reshape(B, L, MLA_HEADS * MLA_V)
    return o @ w_o, ckv, kpe


def swiglu(h, w_gu, w_down):
    gu = h @ w_gu
    return (jax.nn.silu(gu[..., :D_FF]) * gu[..., D_FF:]) @ w_down


def trunk(x, pos0, lead, lru_h, lru_conv, dn_S, dn_conv, pool_ckv, pool_kpe, page_table, P):
    names = ('lru_h', 'lru_conv', 'dn_S', 'dn_conv', 'ckv', 'kpe')
    new = {n: [] for n in names}
    for i in range(DEPTH):
        kind, j = i % N_MIXERS, i // N_MIXERS
        h = rmsnorm(x, P['norm_mix'][i])
        if kind == 0:
            y, hN, cb = lru_mixer(h, lru_h[j], lru_conv[j], P['lru_w_in'][j], P['lru_conv_w'][j], P['lru_conv_b'][j],
                                  P['lru_w_a'][j], P['lru_b_a'][j], P['lru_w_i'][j], P['lru_b_i'][j],
                                  P['lru_lambda'][j], P['lru_w_out'][j])
            new['lru_h'].append(hN)
            new['lru_conv'].append(cb)
        elif kind == 1:
            y, SN, cb = deltanet_mixer(h, dn_S[j], dn_conv[j], lead, P['dn_w_in'][j], P['dn_conv_w'][j],
                                       P['dn_a_log'][j], P['dn_dt_bias'][j], P['dn_norm'][j], P['dn_w_out'][j])
            new['dn_S'].append(SN)
            new['dn_conv'].append(cb)
        else:
            pc = None if pool_ckv is None else pool_ckv[j]
            pk = None if pool_kpe is None else pool_kpe[j]
            y, ckv, kpe = mla_mixer(h, pos0, pc, pk, page_table, P['mla_w_dq'][j], P['mla_q_norm'][j],
                                    P['mla_w_uq'][j], P['mla_w_dkv'][j], P['mla_kv_norm'][j],
                                    P['mla_w_uk'][j], P['mla_w_uv'][j], P['mla_w_o'][j])
            new['ckv'].append(ckv)
            new['kpe'].append(kpe)
        x = x + y
        x = x + swiglu(rmsnorm(x, P['norm_ffn'][i]), P['ffn_w_gu'][i], P['ffn_w_down'][i])
    return rmsnorm(x, P['norm_final']), {n: jnp.stack(new[n]) for n in names}


def setup_inputs(seed: int = 0) -> dict:
    key = jax.random.key(seed)
    keys = jax.random.split(key, 64)
    ctr = [0]

    def nk():
        ctr[0] += 1
        return keys[ctr[0] - 1]

    def nrm(shape, scale=1.0):
        return jax.random.normal(nk(), shape, F32) * scale

    def unif(shape, lo, hi):
        return jax.random.uniform(nk(), shape, F32, lo, hi)

    def gain(shape):
        return 1.0 + nrm(shape, 0.02)

    n_pages = PAST_LEN // PAGE_SIZE
    n_used = DEC_BATCH * n_pages
    n_pool = n_used + n_used // 4
    perm = jax.random.permutation(nk(), n_pool)
    page_table = perm[:n_used].reshape(DEC_BATCH, n_pages).astype(jnp.int32)

    s = unif((N_LRU, LRU_WIDTH), 0.9, 0.999) ** (1.0 / LRU_C)
    lru_lambda = jnp.log(s) - jnp.log1p(-s)
    dt = jnp.exp(unif((N_DN, DN_HEADS), math.log(1e-3), math.log(1e-1)))
    dn_dt_bias = dt + jnp.log(-jnp.expm1(-dt))
    dn_a_log = jnp.log(unif((N_DN, DN_HEADS), 1.0, 16.0))
    D = D_MODEL
    return {
        'x_prompt': nrm((BATCH, SEQ, D)),
        'x_sample': nrm((DEC_BATCH, DEC_SEQ, D)),
        'state_lru_h': nrm((N_LRU, DEC_BATCH, LRU_WIDTH), 0.5),
        'state_lru_conv': nrm((N_LRU, DEC_BATCH, CONV_W - 1, LRU_WIDTH)),
        'state_dn_S': nrm((N_DN, DEC_BATCH, DN_HEADS, DN_DK, DN_DV), DN_DK ** -0.5),
        'state_dn_conv': nrm((N_DN, DEC_BATCH, CONV_W - 1, DN_QKV)),
        'cache_mla_ckv': nrm((N_MLA, n_pool, PAGE_SIZE, MLA_KV_RANK)),
        'cache_mla_kpe': nrm((N_MLA, n_pool, PAGE_SIZE, MLA_ROPE)),
        'page_table': page_table,
        'meta_tokens': nrm((N_META, D)),
        'norm_mix': gain((DEPTH, D)),
        'norm_ffn': gain((DEPTH, D)),
        'norm_final': gain((D,)),
        'lru_w_in': nrm((N_LRU, D, 2 * LRU_WIDTH), D ** -0.5),
        'lru_conv_w': nrm((N_LRU, CONV_W, LRU_WIDTH), CONV_W ** -0.5),
        'lru_conv_b': nrm((N_LRU, LRU_WIDTH), 0.01),
        'lru_w_a': nrm((N_LRU, LRU_BLOCKS, LRU_BW, LRU_BW), LRU_BW ** -0.5),
        'lru_b_a': nrm((N_LRU, LRU_WIDTH), 0.1),
        'lru_w_i': nrm((N_LRU, LRU_BLOCKS, LRU_BW, LRU_BW), LRU_BW ** -0.5),
        'lru_b_i': nrm((N_LRU, LRU_WIDTH), 0.1),
        'lru_lambda': lru_lambda,
        'lru_w_out': nrm((N_LRU, LRU_WIDTH, D), LRU_WIDTH ** -0.5),
        'dn_w_in': nrm((N_DN, D, DN_QKV + DN_HEADS * DN_DV + 2 * DN_HEADS), D ** -0.5),
        'dn_conv_w': nrm((N_DN, CONV_W, DN_QKV), CONV_W ** -0.5),
        'dn_a_log': dn_a_log,
        'dn_dt_bias': dn_dt_bias,
        'dn_norm': gain((N_DN, DN_DV)),
        'dn_w_out': nrm((N_DN, DN_HEADS * DN_DV, D), (DN_HEADS * DN_DV) ** -0.5),
        'mla_w_dq': nrm((N_MLA, D, MLA_Q_RANK), D ** -0.5),
        'mla_q_norm': gain((N_MLA, MLA_Q_RANK)),
        'mla_w_uq': nrm((N_MLA, MLA_Q_RANK, MLA_HEADS * (MLA_NOPE + MLA_ROPE)), MLA_Q_RANK ** -0.5),
        'mla_w_dkv': nrm((N_MLA, D, MLA_KV_RANK + MLA_ROPE), D ** -0.5),
        'mla_kv_norm': gain((N_MLA, MLA_KV_RANK)),
        'mla_w_uk': nrm((N_MLA, MLA_KV_RANK, MLA_HEADS, MLA_NOPE), MLA_KV_RANK ** -0.5),
        'mla_w_uv': nrm((N_MLA, MLA_KV_RANK, MLA_HEADS, MLA_V), MLA_KV_RANK ** -0.5),
        'mla_w_o': nrm((N_MLA, MLA_HEADS * MLA_V, D), (MLA_HEADS * MLA_V) ** -0.5),
        'ffn_w_gu': nrm((DEPTH, D, 2 * D_FF), D ** -0.5),
        'ffn_w_down': nrm((DEPTH, D_FF, D), D_FF ** -0.5),
    }


def reference(x_prompt, x_sample, state_lru_h, state_lru_conv, state_dn_S, state_dn_conv,
              cache_mla_ckv, cache_mla_kpe, page_table, meta_tokens, norm_mix, norm_ffn, norm_final,
              lru_w_in, lru_conv_w, lru_conv_b, lru_w_a, lru_b_a, lru_w_i, lru_b_i, lru_lambda, lru_w_out,
              dn_w_in, dn_conv_w, dn_a_log, dn_dt_bias, dn_norm, dn_w_out,
              mla_w_dq, mla_q_norm, mla_w_uq, mla_w_dkv, mla_kv_norm, mla_w_uk, mla_w_uv, mla_w_o,
              ffn_w_gu, ffn_w_down):
    P = {
        'norm_mix': norm_mix, 'norm_ffn': norm_ffn, 'norm_final': norm_final,
        'lru_w_in': lru_w_in, 'lru_conv_w': lru_conv_w, 'lru_conv_b': lru_conv_b,
        'lru_w_a': lru_w_a, 'lru_b_a': lru_b_a, 'lru_w_i': lru_w_i, 'lru_b_i': lru_b_i,
        'lru_lambda': lru_lambda, 'lru_w_out': lru_w_out,
        'dn_w_in': dn_w_in, 'dn_conv_w': dn_conv_w, 'dn_a_log': dn_a_log, 'dn_dt_bias': dn_dt_bias,
        'dn_norm': dn_norm, 'dn_w_out': dn_w_out,
        'mla_w_dq': mla_w_dq, 'mla_q_norm': mla_q_norm, 'mla_w_uq': mla_w_uq, 'mla_w_dkv': mla_w_dkv,
        'mla_kv_norm': mla_kv_norm, 'mla_w_uk': mla_w_uk, 'mla_w_uv': mla_w_uv, 'mla_w_o': mla_w_o,
        'ffn_w_gu': ffn_w_gu, 'ffn_w_down': ffn_w_down,
    }
    dt = x_prompt.dtype
    B = x_prompt.shape[0]
    meta = jnp.broadcast_to(meta_tokens.astype(dt)[None], (B, N_META, D_MODEL))
    x_full = jnp.concatenate([meta, x_prompt], axis=1)
    z_h = jnp.zeros((N_LRU, B, LRU_WIDTH), dt)
    z_c = jnp.zeros((N_LRU, B, CONV_W - 1, LRU_WIDTH), dt)
    z_S = jnp.zeros((N_DN, B, DN_HEADS, DN_DK, DN_DV), dt)
    z_dc = jnp.zeros((N_DN, B, CONV_W - 1, DN_QKV), dt)
    y_full, pn = trunk(x_full, 0, N_META, z_h, z_c, z_S, z_dc, None, None, None, P)
    y_prompt = y_full[:, N_META:]
    past_len = page_table.shape[1] * cache_mla_ckv.shape[2]
    y_sample, sn = trunk(x_sample, past_len, 0, state_lru_h, state_lru_conv, state_dn_S, state_dn_conv,
                         cache_mla_ckv, cache_mla_kpe, page_table, P)
    return (y_prompt, y_sample,
            pn['lru_h'], pn['lru_conv'], pn['dn_S'], pn['dn_conv'], pn['ckv'], pn['kpe'],
            sn['lru_h'], sn['lru_conv'], sn['dn_S'], sn['dn_conv'], sn['ckv'], sn['kpe'])
```

```python
import functools
import math

import jax
import jax.numpy as jnp
from jax import lax
from jax.experimental import pallas as pl
from jax.experimental.pallas import tpu as pltpu

F32 = jnp.float32
BF16 = jnp.bfloat16
HIGHEST = lax.Precision.HIGHEST

RMS_EPS = 1e-6
L2_EPS = 1e-6
N_META = 16
CONV_W = 4
LRU_C = 8.0
LRU_BLOCKS = 4
DN_HEADS = 8
DN_DK = 128
DN_DV = 128
DN_CHUNK = 64
MLA_HEADS = 8
MLA_KV_RANK = 256
MLA_NOPE = 128
MLA_ROPE = 64
MLA_V = 128
MLA_SCALE = (MLA_NOPE + MLA_ROPE) ** -0.5
ROPE_THETA = 10000.0
MLA_KEY_CHUNK = 256
PAGES_PER_STEP = 8
SUB = 8
DN_PAD = DN_CHUNK - N_META


def _cparams(vmem_mb=None, sem=None):
    kw = {}
    if vmem_mb is not None:
        kw["vmem_limit_bytes"] = vmem_mb << 20
    if sem is not None:
        kw["dimension_semantics"] = sem
    return pltpu.CompilerParams(**kw)


def _const_spec(shape):
    nd = len(shape)
    return pl.BlockSpec(shape, lambda *_: (0,) * nd, pipeline_mode=pl.Buffered(1))


def _row_spec(tm, n):
    return pl.BlockSpec((tm, n), lambda i: (i, 0))


def _pick_tile(n, target):
    best = None
    for t in range(SUB, min(n, target) + 1, SUB):
        if n % t == 0:
            best = t
    return best if best is not None else n


def _rms(x, g):
    return x * lax.rsqrt(jnp.mean(x * x, axis=-1, keepdims=True) + RMS_EPS) * g


def _softplus(x):
    return jnp.maximum(x, 0.0) + jnp.log1p(jnp.exp(-jnp.abs(x)))


def _dot(a, b):
    return jnp.dot(a, b, preferred_element_type=F32)


def _dot_nt(a, b):
    return lax.dot_general(a, b, (((1,), (1,)), ((), ())), preferred_element_type=F32)


def _shift_rows(x, d, fill, row):
    return jnp.where(row >= d, pltpu.roll(x, d, axis=0), fill)


def _ffn_body(*refs, final):
    if final:
        x_ref, g_ref, wg_ref, wu_ref, wd_ref, gf_ref, o_ref = refs
    else:
        x_ref, g_ref, wg_ref, wu_ref, wd_ref, o_ref = refs
    x = x_ref[...]
    h = _rms(x, g_ref[...]).astype(BF16)
    gate = _dot(h, wg_ref[...])
    up = _dot(h, wu_ref[...])
    act = (jax.nn.silu(gate) * up).astype(BF16)
    y = x + _dot(act, wd_ref[...])
    if final:
        y = _rms(y, gf_ref[...])
    o_ref[...] = y


def _ffn(x, g, wg, wu, wd, g_final=None, tm_target=344):
    m, d = x.shape
    ff = wg.shape[1]
    tm = _pick_tile(m, tm_target)
    final = g_final is not None
    args = [x, g, wg, wu, wd] + ([g_final] if final else [])
    in_specs = [_row_spec(tm, d), _const_spec((1, d)), _const_spec((d, ff)),
                _const_spec((d, ff)), _const_spec((ff, d))]
    if final:
        in_specs.append(_const_spec((1, d)))
    return pl.pallas_call(
        functools.partial(_ffn_body, final=final),
        grid=(m // tm,),
        in_specs=in_specs,
        out_specs=_row_spec(tm, d),
        out_shape=jax.ShapeDtypeStruct((m, d), F32),
        compiler_params=_cparams(56, ("parallel",)),
        name="ffn",
    )(*args)


def _mm_res_body(x_ref, a_ref, w_ref, o_ref):
    o_ref[...] = x_ref[...] + _dot(a_ref[...].astype(BF16), w_ref[...])


def _mm_res(x, a, w, tm_target=688):
    m, d = x.shape
    k = a.shape[1]
    tm = _pick_tile(m, tm_target)
    return pl.pallas_call(
        _mm_res_body,
        grid=(m // tm,),
        in_specs=[_row_spec(tm, d), _row_spec(tm, k), _const_spec((k, d))],
        out_specs=_row_spec(tm, d),
        out_shape=jax.ShapeDtypeStruct((m, d), F32),
        compiler_params=_cparams(40, ("parallel",)),
        name="mm_res",
    )(x, a, w)


def _lru_in_body(x_ref, g_ref, wg_ref, wx_ref, gate_ref, xl_ref):
    h = _rms(x_ref[...], g_ref[...]).astype(BF16)
    gate_ref[...] = jax.nn.gelu(_dot(h, wg_ref[...]))
    xl_ref[...] = _dot(h, wx_ref[...])


def _lru_in(x, g, wg, wx, tm_target=688):
    m, d = x.shape
    w = wg.shape[1]
    tm = _pick_tile(m, tm_target)
    return pl.pallas_call(
        _lru_in_body,
        grid=(m // tm,),
        in_specs=[_row_spec(tm, d), _const_spec((1, d)), _const_spec((d, w)), _const_spec((d, w))],
        out_specs=[_row_spec(tm, w), _row_spec(tm, w)],
        out_shape=[jax.ShapeDtypeStruct((m, w), F32)] * 2,
        compiler_params=_cparams(40, ("parallel",)),
        name="lru_in",
    )(x, g, wg, wx)


def _lru_gates(xc, wa_ref, ba, wi_ref, bi, lam):
    nb, bw, _ = wa_ref.shape
    xb = xc.astype(BF16)
    ra = jnp.concatenate([_dot(xb[:, n * bw:(n + 1) * bw], wa_ref[n]) for n in range(nb)], axis=1)
    ri = jnp.concatenate([_dot(xb[:, n * bw:(n + 1) * bw], wi_ref[n]) for n in range(nb)], axis=1)
    r = jax.nn.sigmoid(ra + ba)
    i = jax.nn.sigmoid(ri + bi)
    log_a = (-LRU_C * _softplus(-lam)) * r
    a = jnp.exp(log_a)
    u = jnp.sqrt(-jnp.tanh(log_a) * (a * a + 1.0)) * (i * xc)
    return a, u


def _lru_scan_body(xl_ref, gate_ref, x_ref, h0_ref, buf_ref, cw_ref, cb_ref, wa_ref, ba_ref,
                   wi_ref, bi_ref, lam_ref, wo_ref, xo_ref, hn_ref, nb_ref, ext_ref, hc_ref, *, tl):
    j = pl.program_id(1)

    @pl.when(j == 0)
    def _():
        ext_ref[0:SUB, :] = buf_ref[...]
        hc_ref[...] = h0_ref[...]

    xl = xl_ref[...]
    ext_ref[SUB:SUB + tl, :] = xl
    cw = cw_ref[...]
    xc = (xl * cw[3:4] + ext_ref[SUB - 1:SUB - 1 + tl, :] * cw[2:3]
          + ext_ref[SUB - 2:SUB - 2 + tl, :] * cw[1:2]
          + ext_ref[SUB - 3:SUB - 3 + tl, :] * cw[0:1] + cb_ref[...])
    ext_ref[0:SUB, :] = xl[tl - SUB:tl]
    a, u = _lru_gates(xc, wa_ref, ba_ref[...], wi_ref, bi_ref[...], lam_ref[...])
    row = lax.broadcasted_iota(jnp.int32, a.shape, 0)
    d = 1
    while d < tl:
        u = a * _shift_rows(u, d, 0.0, row) + u
        a = a * _shift_rows(a, d, 1.0, row)
        d *= 2
    hs = a * hc_ref[...] + u
    last = hs[tl - 1:tl]
    hc_ref[...] = last
    hn_ref[...] = last
    nb_ref[...] = xl[tl - SUB:tl]
    xo_ref[...] = x_ref[...] + _dot((hs * gate_ref[...]).astype(BF16), wo_ref[...])


def _lru_scan(xl, gate, x, h0, buf8, p, tl_target=344):
    b, l, w = xl.shape
    d = x.shape[2]
    tl = _pick_tile(l, tl_target)
    nb, bw, _ = p["wa"].shape
    seq = pl.BlockSpec((None, tl, w), lambda i, j: (i, j, 0))
    per_b = lambda r: pl.BlockSpec((None, r, w), lambda i, j: (i, 0, 0))
    return pl.pallas_call(
        functools.partial(_lru_scan_body, tl=tl),
        grid=(b, l // tl),
        in_specs=[seq, seq, pl.BlockSpec((None, tl, d), lambda i, j: (i, j, 0)),
                  per_b(1), per_b(SUB),
                  _const_spec((CONV_W, w)), _const_spec((1, w)),
                  _const_spec((nb, bw, bw)), _const_spec((1, w)),
                  _const_spec((nb, bw, bw)), _const_spec((1, w)),
                  _const_spec((1, w)), _const_spec((w, d))],
        out_specs=[pl.BlockSpec((None, tl, d), lambda i, j: (i, j, 0)), per_b(1), per_b(SUB)],
        out_shape=[jax.ShapeDtypeStruct((b, l, d), F32), jax.ShapeDtypeStruct((b, 1, w), F32),
                   jax.ShapeDtypeStruct((b, SUB, w), F32)],
        scratch_shapes=[pltpu.VMEM((tl + SUB, w), F32), pltpu.VMEM((1, w), F32)],
        compiler_params=_cparams(48, ("parallel", "arbitrary")),
        name="lru_scan",
    )(xl, gate, x, h0, buf8, p["cw"], p["cb"], p["wa"], p["ba"], p["wi"], p["bi"], p["lam"], p["wo"])


def _lru_step_body(xl_ref, gate_ref, x_ref, h0_ref, buf_ref, cw_ref, cb_ref, wa_ref, ba_ref,
                   wi_ref, bi_ref, lam_ref, wo_ref, xo_ref, hn_ref, nb_ref):
    xl = xl_ref[...]
    cw = cw_ref[...]
    xc = (xl * cw[3:4] + buf_ref[2] * cw[2:3] + buf_ref[1] * cw[1:2] + buf_ref[0] * cw[0:1]
          + cb_ref[...])
    a, u = _lru_gates(xc, wa_ref, ba_ref[...], wi_ref, bi_ref[...], lam_ref[...])
    hs = a * h0_ref[...] + u
    hn_ref[...] = hs
    nb_ref[0] = buf_ref[1]
    nb_ref[1] = buf_ref[2]
    nb_ref[2] = xl
    xo_ref[...] = x_ref[...] + _dot((hs * gate_ref[...]).astype(BF16), wo_ref[...])


def _lru_step(xl, gate, x, h0, buf_t, p):
    m, w = xl.shape
    d = x.shape[1]
    vm = pl.BlockSpec(memory_space=pltpu.VMEM)
    return pl.pallas_call(
        _lru_step_body,
        in_specs=[vm] * 13,
        out_specs=[vm] * 3,
        out_shape=[jax.ShapeDtypeStruct((m, d), F32), jax.ShapeDtypeStruct((m, w), F32),
                   jax.ShapeDtypeStruct((CONV_W - 1, m, w), F32)],
        compiler_params=_cparams(32),
        name="lru_step",
    )(xl, gate, x, h0, buf_t, p["cw"], p["cb"], p["wa"], p["ba"], p["wi"], p["bi"], p["lam"], p["wo"])


def _dn_in_body(x_ref, g_ref, wqkv_ref, wz_ref, wba_ref, alog_ref, dtb_ref, qkv_ref, z_ref, gb_ref):
    h = _rms(x_ref[...], g_ref[...]).astype(BF16)
    qkv_ref[...] = _dot(h, wqkv_ref[...])
    z_ref[...] = _dot(h, wz_ref[...])
    ba = _dot(h, wba_ref[...])
    col = lax.broadcasted_iota(jnp.int32, ba.shape, 1)
    beta = jax.nn.sigmoid(ba)
    g = -jnp.exp(alog_ref[...]) * _softplus(ba + dtb_ref[...])
    gb_ref[...] = jnp.where(col < DN_HEADS, beta, g)


def _dn_in(x, g, p, tm_target=688):
    m, d = x.shape
    nq = p["wqkv"].shape[1]
    nz = p["wz"].shape[1]
    tm = _pick_tile(m, tm_target)
    return pl.pallas_call(
        _dn_in_body,
        grid=(m // tm,),
        in_specs=[_row_spec(tm, d), _const_spec((1, d)), _const_spec((d, nq)), _const_spec((d, nz)),
                  _const_spec((d, 128)), _const_spec((1, 128)), _const_spec((1, 128))],
        out_specs=[_row_spec(tm, nq), _row_spec(tm, nz), _row_spec(tm, 128)],
        out_shape=[jax.ShapeDtypeStruct((m, nq), F32), jax.ShapeDtypeStruct((m, nz), F32),
                   jax.ShapeDtypeStruct((m, 128), F32)],
        compiler_params=_cparams(56, ("parallel",)),
        name="dn_in",
    )(x, g, p["wqkv"], p["wz"], p["wba"], p["alog"], p["dtb"])


def _l2norm(x):
    return x * lax.rsqrt(jnp.sum(x * x, axis=-1, keepdims=True) + L2_EPS)


def _dot_hi(a, b):
    return jnp.dot(a, b, precision=HIGHEST, preferred_element_type=F32)


def _dn_prep_body(q_ref, k_ref, v_ref, gb_ref, bq_ref, bk_ref, bv_ref, cwq_ref, cwk_ref, cwv_ref,
                  u_ref, w_ref, qd_ref, kdt_ref, at_ref, dec_ref, nbq_ref, nbk_ref, nbv_ref,
                  ext_ref, cq_ref, ck_ref, cv_ref, gc_ref, bc_ref, *, l, grp):
    c = DN_CHUNK
    r_tot = l + DN_PAD
    n_chunks = r_tot // c
    hd = pl.program_id(1)
    rowp = lax.broadcasted_iota(jnp.int32, (r_tot, 128), 0)
    valid = rowp >= DN_PAD

    def conv(src_ref, buf_ref, cw_ref, nb_ref):
        ext_ref[0:DN_PAD, :] = jnp.zeros((DN_PAD, 128), F32)
        ext_ref[DN_PAD:DN_PAD + SUB, :] = buf_ref[...]
        ext_ref[DN_PAD + SUB:DN_PAD + SUB + l, :] = src_ref[...]
        cw = cw_ref[...]
        y = (ext_ref[SUB:SUB + r_tot, :] * cw[3:4] + ext_ref[SUB - 1:SUB - 1 + r_tot, :] * cw[2:3]
             + ext_ref[SUB - 2:SUB - 2 + r_tot, :] * cw[1:2]
             + ext_ref[SUB - 3:SUB - 3 + r_tot, :] * cw[0:1])
        nb_ref[...] = ext_ref[r_tot:r_tot + SUB, :]
        return jnp.where(valid, jax.nn.silu(y), 0.0)

    cq_ref[...] = _l2norm(conv(q_ref, bq_ref, cwq_ref, nbq_ref)) * (DN_DK ** -0.5)
    ck_ref[...] = _l2norm(conv(k_ref, bk_ref, cwk_ref, nbk_ref))
    cv_ref[...] = conv(v_ref, bv_ref, cwv_ref, nbv_ref)

    gb = gb_ref[...]
    lane = lax.broadcasted_iota(jnp.int32, gb.shape, 1)
    beta_col = jnp.sum(jnp.where(lane == hd, gb, 0.0), axis=-1, keepdims=True)
    g_col = jnp.sum(jnp.where(lane == hd + DN_HEADS, gb, 0.0), axis=-1, keepdims=True)
    gc_ref[0:DN_PAD, :] = jnp.zeros((DN_PAD, 128), F32)
    bc_ref[0:DN_PAD, :] = jnp.zeros((DN_PAD, 128), F32)
    gc_ref[DN_PAD:r_tot, :] = jnp.broadcast_to(g_col, (l, 128))
    bc_ref[DN_PAD:r_tot, :] = jnp.broadcast_to(beta_col, (l, 128))

    rows = grp * c
    ri = lax.broadcasted_iota(jnp.int32, (c, c), 0)
    ci = lax.broadcasted_iota(jnp.int32, (c, c), 1)
    tril = ci <= ri
    strict = ci < ri
    eye = (ci == ri).astype(F32)
    rin = lax.broadcasted_iota(jnp.int32, (rows, 128), 0) & (c - 1)

    def group(gi, carry):
        r0 = pl.multiple_of(gi * rows, rows)
        gflat = gc_ref[pl.ds(r0, rows), :]
        d = 1
        while d < c:
            gflat = gflat + jnp.where(rin >= d, pltpu.roll(gflat, d, axis=0), 0.0)
            d *= 2
        for ch in range(grp):
            rs = pl.ds(r0 + ch * c, c)
            gcum = gflat[ch * c:(ch + 1) * c]
            k2 = ck_ref[rs, :]
            q2 = cq_ref[rs, :]
            b2 = bc_ref[rs, :]
            decay = jnp.exp(jnp.where(tril, gcum[:, :c] - gcum.T[:c, :], -jnp.inf))
            kb = k2.astype(BF16)
            n = jnp.where(strict, b2[:, :c] * _dot_nt(kb, kb) * decay, 0.0)
            t = eye - n
            pw = _dot_hi(n, n)
            span = 2
            while True:
                t = t + _dot_hi(t, pw)
                span *= 2
                if span >= c:
                    break
                pw = _dot_hi(pw, pw)
            eg = jnp.exp(gcum)
            u_ref[rs, :] = _dot_hi(t, cv_ref[rs, :] * b2)
            w_ref[rs, :] = _dot_hi(t, k2 * (b2 * eg)).astype(BF16)
            at_ref[rs, :] = (_dot_nt(q2.astype(BF16), kb) * decay).astype(BF16)
            qd_ref[rs, :] = (q2 * eg).astype(BF16)
            glast = gcum[c - 1:c, :]
            kdt_ref[pl.ds((gi * grp + ch) * 128, 128), :] = (k2 * jnp.exp(glast - gcum)).T.astype(BF16)
            dec_ref[gi * grp + ch] = jnp.broadcast_to(jnp.exp(glast), (SUB, 128))
        return carry

    lax.fori_loop(0, n_chunks // grp, group, 0)


def _dn_prep(qkv, gb, buf8, cw, grp):
    b, l, _ = qkv.shape
    hh = DN_HEADS
    r_tot = l + DN_PAD
    n_chunks = r_tot // DN_CHUNK
    slab = lambda off: pl.BlockSpec((None, l, 128), lambda i, h: (i, 0, off + h))
    bslab = lambda off: pl.BlockSpec((None, SUB, 128), lambda i, h: (i, 0, off + h))
    cslab = lambda off: pl.BlockSpec((CONV_W, 128), lambda i, h: (0, off + h))
    oslab = lambda r, n: pl.BlockSpec((None, None, r, n), lambda i, h: (i, h, 0, 0))
    bshape = jax.ShapeDtypeStruct((b, SUB, hh * 128), F32)
    return pl.pallas_call(
        functools.partial(_dn_prep_body, l=l, grp=grp),
        grid=(b, hh),
        in_specs=[slab(0), slab(hh), slab(2 * hh),
                  pl.BlockSpec((None, l, 128), lambda i, h: (i, 0, 0)),
                  bslab(0), bslab(hh), bslab(2 * hh), cslab(0), cslab(hh), cslab(2 * hh)],
        out_specs=[oslab(r_tot, 128), oslab(r_tot, 128), oslab(r_tot, 128),
                   oslab(n_chunks * 128, DN_CHUNK), oslab(r_tot, DN_CHUNK),
                   pl.BlockSpec((None, None, n_chunks, SUB, 128), lambda i, h: (i, h, 0, 0, 0)),
                   bslab(0), bslab(0), bslab(0)],
        out_shape=[jax.ShapeDtypeStruct((b, hh, r_tot, 128), F32),
                   jax.ShapeDtypeStruct((b, hh, r_tot, 128), BF16),
                   jax.ShapeDtypeStruct((b, hh, r_tot, 128), BF16),
                   jax.ShapeDtypeStruct((b, hh, n_chunks * 128, DN_CHUNK), BF16),
                   jax.ShapeDtypeStruct((b, hh, r_tot, DN_CHUNK), BF16),
                   jax.ShapeDtypeStruct((b, hh, n_chunks, SUB, 128), F32),
                   bshape, bshape, bshape],
        scratch_shapes=[pltpu.VMEM((r_tot + SUB, 128), F32)] + [pltpu.VMEM((r_tot, 128), F32)] * 5,
        compiler_params=_cparams(48, ("parallel", "arbitrary")),
        name="dn_prep",
    )(qkv, qkv, qkv, gb, buf8, buf8, buf8, cw, cw, cw)


def _dn_rec_body(u_ref, w_ref, qd_ref, kdt_ref, at_ref, dec_ref, s0_ref, o_ref, sn_ref, s_ref, *, cpg):
    c = DN_CHUNK
    gi = pl.program_id(1)

    @pl.when(gi == 0)
    def _():
        s_ref[...] = s0_ref[...]

    def chunk(n, carry):
        sl = pl.ds(pl.multiple_of(n * c, c), c)
        for hd in range(DN_HEADS):
            s = s_ref[hd]
            sb = s.astype(BF16)
            delta = u_ref[hd, sl, :] - _dot(w_ref[hd, sl, :], sb)
            db = delta.astype(BF16)
            o_ref[hd, sl, :] = _dot(qd_ref[hd, sl, :], sb) + _dot(at_ref[hd, sl, :], db)
            upd = _dot(kdt_ref[hd, pl.ds(pl.multiple_of(n * 128, 128), 128), :], db)
            s_ref[hd] = s * dec_ref[hd, n, 0:1, :] + upd
        return carry

    lax.fori_loop(0, cpg, chunk, 0)
    sn_ref[...] = s_ref[...]


def _dn_rec(u, w, qd, kdt, at, dec, s0, cpg):
    b, hh, r_tot, _ = u.shape
    n_chunks = r_tot // DN_CHUNK
    rows = cpg * DN_CHUNK
    blk = lambda n: pl.BlockSpec((None, hh, rows, n), lambda i, j: (i, 0, j, 0))
    sblk = pl.BlockSpec((None, hh, 128, 128), lambda i, j: (i, 0, 0, 0))
    return pl.pallas_call(
        functools.partial(_dn_rec_body, cpg=cpg),
        grid=(b, n_chunks // cpg),
        in_specs=[blk(128), blk(128), blk(128),
                  pl.BlockSpec((None, hh, cpg * 128, DN_CHUNK), lambda i, j: (i, 0, j, 0)),
                  blk(DN_CHUNK),
                  pl.BlockSpec((None, hh, cpg, SUB, 128), lambda i, j: (i, 0, j, 0, 0)), sblk],
        out_specs=[blk(128), sblk],
        out_shape=[jax.ShapeDtypeStruct((b, hh, r_tot, 128), F32),
                   jax.ShapeDtypeStruct((b, hh, 128, 128), F32)],
        scratch_shapes=[pltpu.VMEM((hh, 128, 128), F32)],
        compiler_params=_cparams(48, ("parallel", "arbitrary")),
        name="dn_rec",
    )(u, w, qd, kdt, at, dec, s0)


def _dn_out_body(o_ref, z_ref, x_ref, ng_ref, wo_ref, xo_ref, *, tl):
    j = pl.program_id(1)
    r0 = pl.multiple_of(DN_PAD + j * tl, SUB)
    ng = ng_ref[...]
    parts = []
    for hd in range(DN_HEADS):
        o = _rms(o_ref[hd, pl.ds(r0, tl), :], ng)
        z = z_ref[:, hd * 128:(hd + 1) * 128]
        parts.append((o * jax.nn.silu(z)).astype(BF16))
    xo_ref[...] = x_ref[...] + _dot(jnp.concatenate(parts, axis=1), wo_ref[...])


def _dn_out(o, z, x, ng, wo, tl_target=688):
    b, hh, r_tot, _ = o.shape
    l, d = x.shape[1], x.shape[2]
    tl = _pick_tile(l, tl_target)
    seq = lambda n: pl.BlockSpec((None, tl, n), lambda i, j: (i, j, 0))
    return pl.pallas_call(
        functools.partial(_dn_out_body, tl=tl),
        grid=(b, l // tl),
        in_specs=[pl.BlockSpec((None, hh, r_tot, 128), lambda i, j: (i, 0, 0, 0)),
                  seq(hh * 128), seq(d), _const_spec((1, 128)), _const_spec((hh * 128, d))],
        out_specs=seq(d),
        out_shape=jax.ShapeDtypeStruct((b, l, d), F32),
        compiler_params=_cparams(48, ("parallel", "arbitrary")),
        name="dn_out",
    )(o, z, x, ng, wo)


def _dn_step_body(qkv_ref, z_ref, gb_ref, buf_ref, s_ref, cw_ref, ng_ref, o_ref, sn_ref, nb_ref):
    hh = DN_HEADS
    x = qkv_ref[...]
    cw = cw_ref[...]
    y = (x * cw[3:4] + buf_ref[2:3, :] * cw[2:3] + buf_ref[1:2, :] * cw[1:2]
         + buf_ref[0:1, :] * cw[0:1])
    nb_ref[0:2, :] = buf_ref[1:3, :]
    nb_ref[2:3, :] = x
    y = jax.nn.silu(y)
    gb = gb_ref[...]
    ng = ng_ref[...]
    for hd in range(hh):
        q = _l2norm(y[:, hd * 128:(hd + 1) * 128]) * (DN_DK ** -0.5)
        k = _l2norm(y[:, (hh + hd) * 128:(hh + hd + 1) * 128])
        v = y[:, (2 * hh + hd) * 128:(2 * hh + hd + 1) * 128]
        beta = gb[:, hd:hd + 1]
        eg = jnp.exp(gb[:, hh + hd:hh + hd + 1])
        s = s_ref[hd]
        sb = s.astype(BF16)
        delta = v * beta - _dot((k * (beta * eg)).astype(BF16), sb)
        attn = jnp.sum(q * k, axis=-1, keepdims=True)
        o = _dot((q * eg).astype(BF16), sb) + attn * delta
        kcol = jnp.transpose(jnp.broadcast_to(k, (128, 128)))
        sn_ref[hd] = s * eg + kcol * delta
        z = z_ref[:, hd * 128:(hd + 1) * 128]
        o_ref[:, hd * 128:(hd + 1) * 128] = _rms(o, ng) * jax.nn.silu(z)


def _dn_step(qkv, z, gb, buf, s0, cw, ng):
    b = qkv.shape[0]
    hh = DN_HEADS
    nq = qkv.shape[2]
    per_b = lambda r, n: pl.BlockSpec((None, r, n), lambda i: (i, 0, 0))
    sblk = pl.BlockSpec((None, hh, 128, 128), lambda i: (i, 0, 0, 0))
    return pl.pallas_call(
        _dn_step_body,
        grid=(b,),
        in_specs=[per_b(1, nq), per_b(1, hh * 128), per_b(1, 128), per_b(CONV_W - 1, nq), sblk,
                  _const_spec((CONV_W, nq)), _const_spec((1, 128))],
        out_specs=[per_b(1, hh * 128), sblk, per_b(CONV_W - 1, nq)],
        out_shape=[jax.ShapeDtypeStruct((b, 1, hh * 128), F32),
                   jax.ShapeDtypeStruct((b, hh, 128, 128), F32),
                   jax.ShapeDtypeStruct((b, CONV_W - 1, nq), F32)],
        compiler_params=_cparams(32, ("parallel",)),
        name="dn_step",
    )(qkv, z, gb, buf, s0, cw, ng)


def _mla_in_body(x_ref, g_ref, wdq_ref, qn_ref, wnope_ref, wpe_ref, wpes_ref, wuk_ref, wkc_ref,
                 kn_ref, wkr_ref, wkrs_ref, cq_ref, sq_ref, ck_ref, sk_ref,
                 qlat_ref, qpe_ref, ckv_ref, kpe_ref):
    hh = MLA_HEADS
    h = _rms(x_ref[...], g_ref[...]).astype(BF16)
    cq = _rms(_dot(h, wdq_ref[...]), qn_ref[...]).astype(BF16)
    q_nope = _dot(cq, wnope_ref[...]).astype(BF16)
    q_pe = _dot(cq, wpe_ref[...]) * cq_ref[...] + _dot(cq, wpes_ref[...]) * sq_ref[...]
    for hd in range(hh):
        qlat_ref[hd] = _dot(q_nope[:, hd * MLA_NOPE:(hd + 1) * MLA_NOPE], wuk_ref[hd])
        qpe_ref[hd] = q_pe[:, hd * MLA_ROPE:(hd + 1) * MLA_ROPE]
    ckv_ref[...] = _rms(_dot(h, wkc_ref[...]), kn_ref[...])
    kpe_ref[...] = _dot(h, wkr_ref[...]) * ck_ref[...] + _dot(h, wkrs_ref[...]) * sk_ref[...]


def _mla_in(x, g, p, tabs, seq_len, tm_target=688):
    m, d = x.shape
    hh = MLA_HEADS
    tm = _pick_tile(seq_len, tm_target)
    nl = seq_len // tm
    qr = p["wdq"].shape[1]
    tab = lambda n: pl.BlockSpec((tm, n), lambda i: (i % nl, 0))
    return pl.pallas_call(
        _mla_in_body,
        grid=(m // tm,),
        in_specs=[_row_spec(tm, d), _const_spec((1, d)), _const_spec((d, qr)), _const_spec((1, qr)),
                  _const_spec((qr, hh * MLA_NOPE)), _const_spec((qr, hh * MLA_ROPE)),
                  _const_spec((qr, hh * MLA_ROPE)), _const_spec((hh, MLA_NOPE, MLA_KV_RANK)),
                  _const_spec((d, MLA_KV_RANK)), _const_spec((1, MLA_KV_RANK)),
                  _const_spec((d, MLA_ROPE)), _const_spec((d, MLA_ROPE)),
                  tab(hh * MLA_ROPE), tab(hh * MLA_ROPE), tab(MLA_ROPE), tab(MLA_ROPE)],
        out_specs=[pl.BlockSpec((hh, tm, MLA_KV_RANK), lambda i: (0, i, 0)),
                   pl.BlockSpec((hh, tm, MLA_ROPE), lambda i: (0, i, 0)),
                   _row_spec(tm, MLA_KV_RANK), _row_spec(tm, MLA_ROPE)],
        out_shape=[jax.ShapeDtypeStruct((hh, m, MLA_KV_RANK), F32),
                   jax.ShapeDtypeStruct((hh, m, MLA_ROPE), F32),
                   jax.ShapeDtypeStruct((m, MLA_KV_RANK), F32),
                   jax.ShapeDtypeStruct((m, MLA_ROPE), F32)],
        compiler_params=_cparams(48, ("parallel",)),
        name="mla_in",
    )(x, g, p["wdq"], p["qn"], p["wnope"], p["wpe"], p["wpes"], p["wuk"], p["wkc"], p["kn"],
      p["wkr"], p["wkrs"], tabs["cq"], tabs["sq"], tabs["ck"], tabs["sk"])


def _mla_attn_body(qlat_ref, qpe_ref, ckv_ref, kpe_ref, o_ref, *, tq, l):
    hh = MLA_HEADS
    tk = MLA_KEY_CHUNK
    n_full = l // tk
    tail = l - n_full * tk
    qi = pl.program_id(1)
    rows = hh * tq
    q = qlat_ref[...].reshape(rows, MLA_KV_RANK).astype(BF16)
    qp = qpe_ref[...].reshape(rows, MLA_ROPE).astype(BF16)
    q_end = (qi + 1) * tq

    def chunk(k0, c, carry):
        m, lsum, acc = carry
        kb = ckv_ref[pl.ds(k0, c), :].astype(BF16)
        kpb = kpe_ref[pl.ds(k0, c), :].astype(BF16)
        s = ((_dot_nt(q, kb) + _dot_nt(qp, kpb)) * MLA_SCALE).reshape(hh, tq, c)
        qpos = qi * tq + lax.broadcasted_iota(jnp.int32, (hh, tq, c), 1)
        kpos = k0 + lax.broadcasted_iota(jnp.int32, (hh, tq, c), 2)
        s = jnp.where(kpos <= qpos, s, -jnp.inf)
        m_new = jnp.maximum(m, jnp.max(s, axis=-1, keepdims=True))
        alpha = jnp.exp(m - m_new)
        pr = jnp.exp(s - m_new)
        lsum = alpha * lsum + jnp.sum(pr, axis=-1, keepdims=True)
        pv = _dot(pr.reshape(rows, c).astype(BF16), kb).reshape(hh, tq, MLA_KV_RANK)
        return m_new, lsum, alpha * acc + pv

    carry = (jnp.full((hh, tq, 1), -jnp.inf, F32), jnp.zeros((hh, tq, 1), F32),
             jnp.zeros((hh, tq, MLA_KV_RANK), F32))
    if n_full > 0:
        n_need = jnp.minimum(n_full, (q_end + tk - 1) // tk)
        carry = lax.fori_loop(
            0, n_need, lambda kc, cr: chunk(pl.multiple_of(kc * tk, tk), tk, cr), carry)
    if tail > 0:
        carry = chunk(n_full * tk, tail, carry)
    _, lsum, acc = carry
    o_ref[...] = acc / lsum


def _mla_attn(qlat, qpe, ckv, kpe, tq_target=344):
    hh, b, l, _ = qlat.shape
    tq = _pick_tile(l, tq_target)
    qblk = lambda n: pl.BlockSpec((hh, None, tq, n), lambda i, j: (0, i, j, 0))
    kblk = lambda n: pl.BlockSpec((None, l, n), lambda i, j: (i, 0, 0))
    return pl.pallas_call(
        functools.partial(_mla_attn_body, tq=tq, l=l),
        grid=(b, l // tq),
        in_specs=[qblk(MLA_KV_RANK), qblk(MLA_ROPE), kblk(MLA_KV_RANK), kblk(MLA_ROPE)],
        out_specs=qblk(MLA_KV_RANK),
        out_shape=jax.ShapeDtypeStruct((hh, b, l, MLA_KV_RANK), F32),
        compiler_params=_cparams(56, ("parallel", "arbitrary")),
        name="mla_attn",
    )(qlat, qpe, ckv, kpe)


def _mla_out_body(ol_ref, x_ref, wuv_ref, wo_ref, xo_ref):
    parts = [_dot(ol_ref[hd].astype(BF16), wuv_ref[hd]).astype(BF16) for hd in range(MLA_HEADS)]
    xo_ref[...] = x_ref[...] + _dot(jnp.concatenate(parts, axis=1), wo_ref[...])


def _mla_out(olat, x, wuv, wo, tm_target=688):
    hh, m, _ = olat.shape
    d = x.shape[1]
    tm = _pick_tile(m, tm_target)
    return pl.pallas_call(
        _mla_out_body,
        grid=(m // tm,),
        in_specs=[pl.BlockSpec((hh, tm, MLA_KV_RANK), lambda i: (0, i, 0)), _row_spec(tm, d),
                  _const_spec((hh, MLA_KV_RANK, MLA_V)), _const_spec((hh * MLA_V, d))],
        out_specs=_row_spec(tm, d),
        out_shape=jax.ShapeDtypeStruct((m, d), F32),
        compiler_params=_cparams(40, ("parallel",)),
        name="mla_out",
    )(olat, x, wuv, wo)


def _mla_decode_body(pt_ref, qlat_ref, qpe_ref, cn_ref, kn_ref, *refs):
    npg = PAGES_PER_STEP
    ck_refs = refs[:npg]
    kp_refs = refs[npg:2 * npg]
    o_ref, m_ref, l_ref, acc_ref = refs[2 * npg:]
    g = pl.program_id(1)
    ql = qlat_ref[...]
    qp = qpe_ref[...]

    @pl.when(g == 0)
    def _():
        s_new = (jnp.sum(ql * cn_ref[...], axis=-1, keepdims=True)
                 + jnp.sum(qp * kn_ref[...], axis=-1, keepdims=True))
        m_ref[...] = s_new * MLA_SCALE
        l_ref[...] = jnp.ones_like(l_ref)
        acc_ref[...] = jnp.broadcast_to(cn_ref[...], acc_ref.shape)

    qb = ql.astype(BF16)
    qpb = qp.astype(BF16)
    kbs = [r[...].astype(BF16) for r in ck_refs]
    s = jnp.concatenate(
        [_dot_nt(qb, kbs[i]) + _dot_nt(qpb, kp_refs[i][...].astype(BF16)) for i in range(npg)],
        axis=1) * MLA_SCALE
    m_old = m_ref[...]
    m_new = jnp.maximum(m_old, jnp.max(s, axis=-1, keepdims=True))
    alpha = jnp.exp(m_old - m_new)
    pr = jnp.exp(s - m_new)
    l_ref[...] = alpha * l_ref[...] + jnp.sum(pr, axis=-1, keepdims=True)
    pb = pr.astype(BF16)
    page = ck_refs[0].shape[0]
    pv = _dot(pb[:, 0:page], kbs[0])
    for i in range(1, npg):
        pv = pv + _dot(pb[:, i * page:(i + 1) * page], kbs[i])
    acc_ref[...] = alpha * acc_ref[...] + pv
    m_ref[...] = m_new

    @pl.when(g == pl.num_programs(1) - 1)
    def _():
        o_ref[...] = acc_ref[...] / l_ref[...]


def _mla_decode(page_table, qlat, qpe, ckv_new, kpe_new, pool_ckv, pool_kpe, layer):
    b, hh, _ = qlat.shape
    n_pages = page_table.shape[1]
    page = pool_ckv.shape[2]
    npg = PAGES_PER_STEP
    per_b = lambda r, n: pl.BlockSpec((None, r, n), lambda i, g, pt: (i, 0, 0))

    def pool_spec(n, k):
        return pl.BlockSpec((None, None, page, n), lambda i, g, pt: (layer, pt[i, g * npg + k], 0, 0))

    grid_spec = pltpu.PrefetchScalarGridSpec(
        num_scalar_prefetch=1,
        grid=(b, n_pages // npg),
        in_specs=[per_b(hh, MLA_KV_RANK), per_b(hh, MLA_ROPE), per_b(1, MLA_KV_RANK), per_b(1, MLA_ROPE)]
        + [pool_spec(MLA_KV_RANK, k) for k in range(npg)]
        + [pool_spec(MLA_ROPE, k) for k in range(npg)],
        out_specs=per_b(hh, MLA_KV_RANK),
        scratch_shapes=[pltpu.VMEM((hh, 1), F32), pltpu.VMEM((hh, 1), F32),
                        pltpu.VMEM((hh, MLA_KV_RANK), F32)],
    )
    return pl.pallas_call(
        _mla_decode_body,
        grid_spec=grid_spec,
        out_shape=jax.ShapeDtypeStruct((b, hh, MLA_KV_RANK), F32),
        compiler_params=_cparams(32, ("parallel", "arbitrary")),
        name="mla_decode",
    )(page_table, qlat, qpe, ckv_new, kpe_new, *([pool_ckv] * npg), *([pool_kpe] * npg))


def _row(v):
    return v.reshape(1, -1).astype(F32)


def _rope_tables(pos):
    half = MLA_ROPE // 2
    freqs = ROPE_THETA ** (-jnp.arange(half, dtype=F32) / half)
    ang = pos.astype(F32)[:, None] * freqs
    c, s = jnp.cos(ang), jnp.sin(ang)
    ck = jnp.concatenate([c, c], axis=1)
    sk = jnp.concatenate([-s, s], axis=1)
    return {"ck": ck, "sk": sk, "cq": jnp.tile(ck, (1, MLA_HEADS)), "sq": jnp.tile(sk, (1, MLA_HEADS))}


def _swap_halves_cols(w):
    k, n = w.shape
    w4 = w.reshape(k, n // MLA_ROPE, 2, MLA_ROPE // 2)
    return w4[:, :, ::-1, :].reshape(k, n)


def _prep_lru(j, lru_w_in, lru_conv_w, lru_conv_b, lru_w_a, lru_b_a, lru_w_i, lru_b_i, lru_lambda, lru_w_out):
    w = lru_w_out.shape[1]
    win = lru_w_in[j].astype(BF16)
    return {"wg": win[:, :w], "wx": win[:, w:], "cw": lru_conv_w[j], "cb": _row(lru_conv_b[j]),
            "wa": lru_w_a[j].astype(BF16), "ba": _row(lru_b_a[j]), "wi": lru_w_i[j].astype(BF16),
            "bi": _row(lru_b_i[j]), "lam": _row(lru_lambda[j]), "wo": lru_w_out[j].astype(BF16)}


def _prep_dn(j, dn_w_in, dn_conv_w, dn_a_log, dn_dt_bias, dn_norm, dn_w_out):
    hh = DN_HEADS
    nq = hh * (2 * DN_DK + DN_DV)
    nz = hh * DN_DV
    win = dn_w_in[j].astype(BF16)
    pad = 128 - 2 * hh
    zeros_h = jnp.zeros((hh,), F32)
    zeros_p = jnp.zeros((pad,), F32)
    return {"wqkv": win[:, :nq], "wz": win[:, nq:nq + nz],
            "wba": jnp.pad(win[:, nq + nz:], ((0, 0), (0, pad))),
            "alog": _row(jnp.concatenate([zeros_h, dn_a_log[j], zeros_p])),
            "dtb": _row(jnp.concatenate([zeros_h, dn_dt_bias[j], zeros_p])),
            "cw": dn_conv_w[j], "ng": _row(dn_norm[j]), "wo": dn_w_out[j].astype(BF16)}


def _prep_mla(j, mla_w_dq, mla_q_norm, mla_w_uq, mla_w_dkv, mla_kv_norm, mla_w_uk, mla_w_uv, mla_w_o):
    hh = MLA_HEADS
    qr = mla_w_uq.shape[1]
    wuq = mla_w_uq[j].astype(BF16).reshape(qr, hh, MLA_NOPE + MLA_ROPE)
    wpe = wuq[:, :, MLA_NOPE:].reshape(qr, hh * MLA_ROPE)
    wdkv = mla_w_dkv[j].astype(BF16)
    wkr = wdkv[:, MLA_KV_RANK:]
    return {"wdq": mla_w_dq[j].astype(BF16), "qn": _row(mla_q_norm[j]),
            "wnope": wuq[:, :, :MLA_NOPE].reshape(qr, hh * MLA_NOPE),
            "wpe": wpe, "wpes": _swap_halves_cols(wpe),
            "wuk": jnp.transpose(mla_w_uk[j].astype(BF16), (1, 2, 0)),
            "wkc": wdkv[:, :MLA_KV_RANK], "kn": _row(mla_kv_norm[j]),
            "wkr": wkr, "wkrs": _swap_halves_cols(wkr),
            "wuv": jnp.transpose(mla_w_uv[j].astype(BF16), (1, 0, 2)),
            "wo": mla_w_o[j].astype(BF16)}


def _pad_hist(buf):
    return jnp.pad(buf, ((0, 0), (SUB - (CONV_W - 1), 0), (0, 0)))


def _lru_layer_prompt(x, g, p, h0, buf):
    b, l, d = x.shape
    gate, xl = _lru_in(x.reshape(b * l, d), g, p["wg"], p["wx"])
    w = gate.shape[1]
    xo, hn, nb = _lru_scan(xl.reshape(b, l, w), gate.reshape(b, l, w), x, h0.reshape(b, 1, w),
                           _pad_hist(buf), p)
    return xo, hn.reshape(b, w), nb[:, SUB - (CONV_W - 1):]


def _lru_layer_sample(x, g, p, h0, buf):
    m, d = x.shape
    gate, xl = _lru_in(x, g, p["wg"], p["wx"])
    xo, hn, nbt = _lru_step(xl, gate, x, h0, jnp.transpose(buf, (1, 0, 2)), p)
    return xo, hn, jnp.transpose(nbt, (1, 0, 2))


def _dn_layer_prompt(x, g, p, s0, buf):
    b, l, d = x.shape
    m = b * l
    qkv, z, gb = _dn_in(x.reshape(m, d), g, p)
    nq = qkv.shape[1]
    n_chunks = (l + DN_PAD) // DN_CHUNK
    grp = max(t for t in (11, 8, 4, 3, 2, 1) if n_chunks % t == 0)
    u, w, qd, kdt, at, dec, nbq, nbk, nbv = _dn_prep(
        qkv.reshape(b, l, nq), gb.reshape(b, l, 128), _pad_hist(buf), p["cw"], grp)
    o, sn = _dn_rec(u, w, qd, kdt, at, dec, s0, grp)
    xo = _dn_out(o, z.reshape(b, l, -1), x, p["ng"], p["wo"])
    nb = jnp.concatenate([nbq, nbk, nbv], axis=2)
    return xo, sn, nb[:, SUB - (CONV_W - 1):]


def _dn_layer_sample(x, g, p, s0, buf):
    m, d = x.shape
    qkv, z, gb = _dn_in(x, g, p)
    o, sn, nb = _dn_step(qkv[:, None, :], z[:, None, :], gb[:, None, :], buf, s0, p["cw"], p["ng"])
    return _mm_res(x, o.reshape(m, -1), p["wo"]), sn, nb


def _mla_layer_prompt(x, g, p):
    b, l, d = x.shape
    m = b * l
    hh = MLA_HEADS
    tabs = _rope_tables(jnp.arange(l, dtype=jnp.int32))
    qlat, qpe, ckv, kpe = _mla_in(x.reshape(m, d), g, p, tabs, l)
    ckv3 = ckv.reshape(b, l, MLA_KV_RANK)
    kpe3 = kpe.reshape(b, l, MLA_ROPE)
    olat = _mla_attn(qlat.reshape(hh, b, l, MLA_KV_RANK), qpe.reshape(hh, b, l, MLA_ROPE), ckv3, kpe3)
    xo = _mla_out(olat.reshape(hh, m, MLA_KV_RANK), x.reshape(m, d), p["wuv"], p["wo"])
    return xo.reshape(b, l, d), ckv3, kpe3


def _mla_layer_sample(x, g, p, pos0, pool_ckv, pool_kpe, page_table, layer):
    m, d = x.shape
    tabs = _rope_tables(jnp.full((m,), pos0, dtype=jnp.int32))
    qlat, qpe, ckv, kpe = _mla_in(x, g, p, tabs, m)
    olat = _mla_decode(page_table, jnp.transpose(qlat, (1, 0, 2)), jnp.transpose(qpe, (1, 0, 2)),
                       ckv[:, None, :], kpe[:, None, :], pool_ckv, pool_kpe, layer)
    xo = _mla_out(jnp.transpose(olat, (1, 0, 2)), x, p["wuv"], p["wo"])
    return xo, ckv[:, None, :], kpe[:, None, :]


def kernel(x_prompt, x_sample, state_lru_h, state_lru_conv, state_dn_S, state_dn_conv, cache_mla_ckv, cache_mla_kpe, page_table, meta_tokens, norm_mix, norm_ffn, norm_final, lru_w_in, lru_conv_w, lru_conv_b, lru_w_a, lru_b_a, lru_w_i, lru_b_i, lru_lambda, lru_w_out, dn_w_in, dn_conv_w, dn_a_log, dn_dt_bias, dn_norm, dn_w_out, mla_w_dq, mla_q_norm, mla_w_uq, mla_w_dkv, mla_kv_norm, mla_w_uk, mla_w_uv, mla_w_o, ffn_w_gu, ffn_w_down):
    depth = norm_mix.shape[0]
    b, seq, d = x_prompt.shape
    db = x_sample.shape[0]
    l = seq + N_META
    d_ff = ffn_w_down.shape[1]
    past_len = page_table.shape[1] * cache_mla_ckv.shape[2]

    lru_args = (lru_w_in, lru_conv_w, lru_conv_b, lru_w_a, lru_b_a, lru_w_i, lru_b_i, lru_lambda, lru_w_out)
    dn_args = (dn_w_in, dn_conv_w, dn_a_log, dn_dt_bias, dn_norm, dn_w_out)
    mla_args = (mla_w_dq, mla_q_norm, mla_w_uq, mla_w_dkv, mla_kv_norm, mla_w_uk, mla_w_uv, mla_w_o)

    xp = jnp.concatenate([jnp.broadcast_to(meta_tokens[None], (b, N_META, d)), x_prompt], axis=1)
    xs = x_sample.reshape(db, d)
    w_lru = lru_w_out.shape[1]
    nq = dn_conv_w.shape[2]

    outs_p = {k: [] for k in ("lru_h", "lru_conv", "dn_S", "dn_conv", "ckv", "kpe")}
    outs_s = {k: [] for k in ("lru_h", "lru_conv", "dn_S", "dn_conv", "ckv", "kpe")}
    for i in range(depth):
        kind, j = i % 3, i // 3
        g = _row(norm_mix[i])
        if kind == 0:
            p = _prep_lru(j, *lru_args)
            xp, hn, cb = _lru_layer_prompt(xp, g, p, jnp.zeros((b, w_lru), F32),
                                           jnp.zeros((b, CONV_W - 1, w_lru), F32))
            outs_p["lru_h"].append(hn)
            outs_p["lru_conv"].append(cb)
            xs, hn, cb = _lru_layer_sample(xs, g, p, state_lru_h[j], state_lru_conv[j])
            outs_s["lru_h"].append(hn)
            outs_s["lru_conv"].append(cb)
        elif kind == 1:
            p = _prep_dn(j, *dn_args)
            xp, sn, cb = _dn_layer_prompt(xp, g, p, jnp.zeros((b, DN_HEADS, DN_DK, DN_DV), F32),
                                          jnp.zeros((b, CONV_W - 1, nq), F32))
            outs_p["dn_S"].append(sn)
            outs_p["dn_conv"].append(cb)
            xs, sn, cb = _dn_layer_sample(xs, g, p, state_dn_S[j], state_dn_conv[j])
            outs_s["dn_S"].append(sn)
            outs_s["dn_conv"].append(cb)
        else:
            p = _prep_mla(j, *mla_args)
            xp, ckv, kpe = _mla_layer_prompt(xp, g, p)
            outs_p["ckv"].append(ckv)
            outs_p["kpe"].append(kpe)
            xs, ckv, kpe = _mla_layer_sample(xs, g, p, past_len, cache_mla_ckv, cache_mla_kpe,
                                             page_table, j)
            outs_s["ckv"].append(ckv)
            outs_s["kpe"].append(kpe)
        wgu = ffn_w_gu[i].astype(BF16)
        wg, wu, wd = wgu[:, :d_ff], wgu[:, d_ff:], ffn_w_down[i].astype(BF16)
        gf = _row(norm_final) if i == depth - 1 else None
        xp = _ffn(xp.reshape(b * l, d), _row(norm_ffn[i]), wg, wu, wd, gf).reshape(b, l, d)
        xs = _ffn(xs, _row(norm_ffn[i]), wg, wu, wd, gf)

    y_prompt = xp[:, N_META:]
    y_sample = xs.reshape(db, 1, d)
    st = lambda v: jnp.stack(v)
    return (y_prompt, y_sample,
            st(outs_p["lru_h"]), st(outs_p["lru_conv"]), st(outs_p["dn_S"]), st(outs_p["dn_conv"]),
            st(outs_p["ckv"]), st(outs_p["kpe"]),
            st(outs_s["lru_h"]), st(outs_s["lru_conv"]), st(outs_s["dn_S"]), st(outs_s["dn_conv"]),
            st(outs_s["ckv"]), st(outs_s["kpe"]))
```
